```python
import math
import jax, jax.numpy as jnp
from jax import lax
import numpy as np

D_MODEL = 2048
BATCH = 4
SEQ = 8192
DEPTH = 2

MIX_WIDTH = D_MODEL
S5_WIDTH = MIX_WIDTH // 4
SGU_WIDTH = MIX_WIDTH // 2
POOL_WIDTH = MIX_WIDTH - S5_WIDTH - SGU_WIDTH
S5_GROUP_CH = 16
S5_GROUPS = S5_WIDTH // S5_GROUP_CH
S5_STATE = 64
DT_MIN = 0.001
DT_MAX = 0.1
CHUNK = 128
SGU_HEAD_DIM = 128
SGU_HEADS = SGU_WIDTH // SGU_HEAD_DIM
POOL_WINDOWS = (2, 4, 8, 16)
POOL_GROUPS = len(POOL_WINDOWS)
POOL_GROUP_CH = POOL_WIDTH // POOL_GROUPS
SPLIT_SIZES = (S5_WIDTH, SGU_WIDTH, SGU_WIDTH, POOL_WIDTH, S5_WIDTH, SGU_WIDTH, POOL_WIDTH)
IN_COLS = sum(SPLIT_SIZES)
SPLIT_POINTS = tuple(int(s) for s in np.cumsum(SPLIT_SIZES)[:-1])
RMS_EPS = 1e-6
LN_EPS = 1e-5

kernel_name = 'hymba_style_s5_gmlp_pool_hybrid'


def rms_norm(x, g):
    xf = x.astype(jnp.float32)
    y = xf * lax.rsqrt(jnp.mean(xf * xf, axis=-1, keepdims=True) + RMS_EPS)
    return (y * g.astype(jnp.float32)).astype(x.dtype)


def s5_mixer(xa, lam_re, lam_im, b_re, b_im, c_re, c_im, d_skip, log_dt, w_glu, b_glu):
    bsz, seq, _ = xa.shape
    f32 = jnp.float32
    xg = xa.astype(f32).reshape(bsz, seq, S5_GROUPS, S5_GROUP_CH)
    lam = lax.complex(lam_re.astype(f32), lam_im.astype(f32))
    dt = jnp.exp(log_dt.astype(f32))[:, None]
    lam_bar = jnp.exp(lam * dt)
    b = lax.complex(b_re.astype(f32), b_im.astype(f32))
    b_bar = ((lam_bar - 1.0) / lam)[..., None] * b
    c = lax.complex(c_re.astype(f32), c_im.astype(f32))
    bu = jnp.einsum('blgh,gph->blgp', xg.astype(jnp.complex64), b_bar)
    a = jnp.broadcast_to(lam_bar, (1, seq) + lam_bar.shape)

    def combine(left, right):
        a_l, b_l = left
        a_r, b_r = right
        return a_r * a_l, a_r * b_l + b_r

    _, states = lax.associative_scan(combine, (a, bu), axis=1)
    y = jnp.einsum('blgp,ghp->blgh', states, c).real + d_skip.astype(f32) * xg
    y = jax.nn.gelu(y.reshape(bsz, seq, S5_WIDTH)).astype(xa.dtype)
    return y * jax.nn.sigmoid(y @ w_glu + b_glu)


def sgu_mixer(u, v, ln_g, ln_b, w_s, b_s):
    bsz, seq, _ = v.shape
    u = jax.nn.gelu(u)
    vf = jax.nn.gelu(v).astype(jnp.float32)
    mu = jnp.mean(vf, axis=-1, keepdims=True)
    var = jnp.mean(jnp.square(vf - mu), axis=-1, keepdims=True)
    vn = ((vf - mu) * lax.rsqrt(var + LN_EPS) * ln_g.astype(jnp.float32)
          + ln_b.astype(jnp.float32)).astype(v.dtype)
    vn = vn.reshape(bsz, seq // CHUNK, CHUNK, SGU_HEADS, SGU_HEAD_DIM)
    causal = jnp.tril(jnp.ones((CHUNK, CHUNK), dtype=bool))
    ws = jnp.where(causal[None], w_s, jnp.zeros_like(w_s))
    s = jnp.einsum('hts,bcshd->bcthd', ws, vn) + jnp.transpose(b_s)[:, :, None]
    return u * s.reshape(bsz, seq, SGU_WIDTH)


def pool_mixer(xc, w_pool, pool_scale):
    bsz, seq, _ = xc.shape
    xg = xc.astype(jnp.float32).reshape(bsz, seq, POOL_GROUPS, POOL_GROUP_CH)
    cs = jnp.cumsum(xg, axis=1)
    pos = jnp.arange(1, seq + 1, dtype=jnp.float32)[None, :, None]
    outs = []
    for g, w in enumerate(POOL_WINDOWS):
        c = cs[:, :, g]
        lagged = jnp.pad(c[:, :seq - w], ((0, 0), (w, 0), (0, 0)))
        mean = (c - lagged) / jnp.minimum(pos, float(w))
        outs.append(mean - xg[:, :, g])
    p = jnp.stack(outs, axis=2).astype(xc.dtype)
    y = jnp.einsum('blgc,gcd->blgd', p, w_pool).reshape(bsz, seq, POOL_WIDTH)
    return y * pool_scale


def setup_inputs(seed: int = 0) -> dict:
    key = jax.random.key(seed)
    ks = jax.random.split(key, 24)
    f32 = jnp.float32
    nrm = lambda k, shape: jax.random.normal(k, shape, dtype=f32)
    L, D = DEPTH, D_MODEL
    G, P, H = S5_GROUPS, S5_STATE, S5_GROUP_CH
    x = nrm(ks[0], (BATCH, SEQ, D))
    norm_g = 1.0 + 0.02 * nrm(ks[1], (L, D))
    w_in = nrm(ks[2], (L, D, IN_COLS)) * D ** -0.5
    lam_re = -0.5 + 0.01 * nrm(ks[3], (L, G, P))
    lam_im = math.pi * jnp.arange(P, dtype=f32)[None, None, :] + 0.01 * nrm(ks[4], (L, G, P))
    b_re = nrm(ks[5], (L, G, P, H)) * (2.0 * H) ** -0.5
    b_im = nrm(ks[6], (L, G, P, H)) * (2.0 * H) ** -0.5
    c_re = nrm(ks[7], (L, G, H, P)) * P ** -0.5
    c_im = nrm(ks[8], (L, G, H, P)) * P ** -0.5
    d_skip = nrm(ks[9], (L, G, H))
    log_dt = jax.random.uniform(ks[10], (L, G), dtype=f32,
                                minval=math.log(DT_MIN), maxval=math.log(DT_MAX))
    w_glu = nrm(ks[11], (L, S5_WIDTH, S5_WIDTH)) * S5_WIDTH ** -0.5
    b_glu = 0.01 * nrm(ks[12], (L, S5_WIDTH))
    ln_g = 1.0 + 0.02 * nrm(ks[13], (L, SGU_WIDTH))
    ln_b = 0.02 * nrm(ks[14], (L, SGU_WIDTH))
    w_s = nrm(ks[15], (L, SGU_HEADS, CHUNK, CHUNK)) * CHUNK ** -0.5
    b_s = 1.0 + 0.02 * nrm(ks[16], (L, SGU_HEADS, CHUNK))
    w_pool = nrm(ks[17], (L, POOL_GROUPS, POOL_GROUP_CH, POOL_GROUP_CH)) * POOL_GROUP_CH ** -0.5
    pool_scale = 1.0 + 0.1 * nrm(ks[18], (L, POOL_WIDTH))
    w_out = nrm(ks[19], (L, MIX_WIDTH, D)) * MIX_WIDTH ** -0.5
    final_g = 1.0 + 0.02 * nrm(ks[20], (D,))
    return {'x': x, 'norm_g': norm_g, 'w_in': w_in, 'lam_re': lam_re, 'lam_im': lam_im,
            'b_re': b_re, 'b_im': b_im, 'c_re': c_re, 'c_im': c_im, 'd_skip': d_skip,
            'log_dt': log_dt, 'w_glu': w_glu, 'b_glu': b_glu, 'ln_g': ln_g, 'ln_b': ln_b,
            'w_s': w_s, 'b_s': b_s, 'w_pool': w_pool, 'pool_scale': pool_scale,
            'w_out': w_out, 'final_g': final_g}


def reference(x, norm_g, w_in, lam_re, lam_im, b_re, b_im, c_re, c_im, d_skip, log_dt,
              w_glu, b_glu, ln_g, ln_b, w_s, b_s, w_pool, pool_scale, w_out, final_g):
    for i in range(DEPTH):
        h = rms_norm(x, norm_g[i])
        z = h @ w_in[i]
        xa, u, v, xc, ga, gb, gc = jnp.split(z, SPLIT_POINTS, axis=-1)
        ya = s5_mixer(xa, lam_re[i], lam_im[i], b_re[i], b_im[i], c_re[i], c_im[i],
                      d_skip[i], log_dt[i], w_glu[i], b_glu[i]) * jax.nn.silu(ga)
        yb = sgu_mixer(u, v, ln_g[i], ln_b[i], w_s[i], b_s[i]) * jax.nn.silu(gb)
        yc = pool_mixer(xc, w_pool[i], pool_scale[i]) * jax.nn.silu(gc)
        y = jnp.concatenate([ya.astype(x.dtype), yb.astype(x.dtype), yc.astype(x.dtype)], axis=-1)
        x = x + y @ w_out[i]
    return rms_norm(x, final_g)
```

```python
import functools
import math

import jax
import jax.numpy as jnp
from jax import lax
from jax.experimental import pallas as pl
from jax.experimental.pallas import tpu as pltpu

F32 = jnp.float32
BF16 = jnp.bfloat16

LANES = 128
SUBLANES = 8
VMEM_LIMIT_BYTES = 56 * 1024 * 1024

S5_GROUP_CH = 16
S5_STATE = 64
CHUNK = 128
SGU_HEAD_DIM = 128
POOL_WINDOWS = (2, 4, 8, 16)
POOL_HALO = 16
RMS_EPS = 1e-6
LN_EPS = 1e-5


def _params(*sem):
    return pltpu.CompilerParams(dimension_semantics=sem, vmem_limit_bytes=VMEM_LIMIT_BYTES)


def _s5_prep_kernel(lre_ref, lim_ref, ldt_ref, bre_ref, bim_ref,
                    lbre_ref, lbim_ref, bbre_ref, bbim_ref):
    lre = lre_ref[...]
    lim = lim_ref[...]
    dt = jnp.exp(ldt_ref[...])
    mag = jnp.exp(lre * dt)
    lbre = mag * jnp.cos(lim * dt)
    lbim = mag * jnp.sin(lim * dt)
    nre = lbre - 1.0
    nim = lbim
    den = lre * lre + lim * lim
    mre = (nre * lre + nim * lim) / den
    mim = (nim * lre - nre * lim) / den
    bre = bre_ref[...]
    bim = bim_ref[...]
    lbre_ref[...] = lbre
    lbim_ref[...] = lbim
    bbre_ref[...] = mre * bre - mim * bim
    bbim_ref[...] = mre * bim + mim * bre


def _s5_prep(lam_re, lam_im, log_dt, b_re, b_im):
    g, p, h = b_re.shape
    rep = lambda a: jnp.repeat(a, h, axis=0)
    lre = rep(lam_re)
    lim = rep(lam_im)
    ldt = jnp.broadcast_to(rep(log_dt[:, None]), (g * h, p))
    bre_t = jnp.transpose(b_re, (0, 2, 1)).reshape(g * h, p)
    bim_t = jnp.transpose(b_im, (0, 2, 1)).reshape(g * h, p)
    shp = jax.ShapeDtypeStruct((g * h, p), F32)
    return pl.pallas_call(_s5_prep_kernel, out_shape=(shp, shp, shp, shp),
                          name="s5_prep")(lre, lim, ldt, bre_t, bim_t)


def _block_diag(blocks):
    nb, r, c = blocks.shape
    eye = jnp.eye(nb, dtype=blocks.dtype)
    return (eye[:, None, :, None] * blocks[:, :, None, :]).reshape(nb * r, nb * c)


def _inproj_kernel(x_ref, g_ref, w_ref, z_ref, h_scr):
    @pl.when(pl.program_id(1) == 0)
    def _():
        x = x_ref[...]
        ms = jnp.mean(x * x, axis=-1, keepdims=True)
        h_scr[...] = (x * lax.rsqrt(ms + RMS_EPS) * g_ref[...]).astype(BF16)

    z_ref[...] = jnp.dot(h_scr[...], w_ref[...], preferred_element_type=F32).astype(z_ref.dtype)


def _inproj(x2d, gain, w_bf16, tm, tn):
    n, d = x2d.shape
    cols = w_bf16.shape[1]
    return pl.pallas_call(
        _inproj_kernel,
        grid=(n // tm, cols // tn),
        in_specs=[pl.BlockSpec((tm, d), lambda i, j: (i, 0)),
                  pl.BlockSpec((1, d), lambda i, j: (0, 0)),
                  pl.BlockSpec((d, tn), lambda i, j: (0, j))],
        out_specs=pl.BlockSpec((tm, tn), lambda i, j: (i, j)),
        out_shape=jax.ShapeDtypeStruct((n, cols), BF16),
        scratch_shapes=[pltpu.VMEM((tm, d), BF16)],
        compiler_params=_params("parallel", "arbitrary"),
        name="inproj",
    )(x2d, gain, w_bf16)


def _s5_kernel(xa_ref, ga_ref, bmat_ref, cmat_ref, lam_ref, dskip_ref, wglu_ref, bglu_ref,
               o_ref, xi_scr, bu_scr, yi_scr, st_scr, *, nb, tt):
    i = pl.program_id(0)
    width = xa_ref.shape[-1]
    half = width // 2
    ncol = st_scr.shape[-1]
    hcol = ncol // 2
    nslab = width // LANES
    nst = ncol // LANES
    lane = lambda k: slice(k * LANES, (k + 1) * LANES)

    @pl.when(i == 0)
    def _():
        st_scr[...] = jnp.zeros_like(st_scr)

    for b in range(nb):
        xb = xa_ref[b].astype(F32)
        for k in range(nslab):
            xi_scr[k, pl.ds(b, tt, stride=nb), :] = xb[:, k * LANES:(k + 1) * LANES]
    xi = jnp.concatenate([xi_scr[k] for k in range(nslab)], axis=1)
    xi_bf = xi.astype(BF16)
    for hf in range(2):
        bu = jnp.dot(xi_bf[:, hf * half:(hf + 1) * half], bmat_ref[hf], preferred_element_type=F32)
        for k in range(nst // 2):
            bu_scr[hf * (nst // 2) + k] = bu[:, lane(k)]

    lam_a = jnp.broadcast_to(lam_ref[0:1, :], (nb, ncol))
    lam_b = jnp.broadcast_to(lam_ref[1:2, :], (nb, ncol))
    q = hcol // 2

    def swap(h):
        return jnp.concatenate([h[:, q:hcol], h[:, 0:q], h[:, hcol + q:], h[:, hcol:hcol + q]], axis=1)

    def step(t, h):
        rows = pl.ds(pl.multiple_of(t * nb, nb), nb)
        bu = jnp.concatenate([bu_scr[k, rows, :] for k in range(nst)], axis=1)
        hn = lam_a * h + lam_b * swap(h) + bu
        for k in range(nst):
            bu_scr[k, rows, :] = hn[:, lane(k)]
        return hn

    st_scr[0:nb, :] = lax.fori_loop(0, tt, step, st_scr[0:nb, :])

    hs = jnp.concatenate([bu_scr[k] for k in range(nst)], axis=1).astype(BF16)
    y = jnp.concatenate(
        [jnp.dot(hs[:, hf * hcol:(hf + 1) * hcol], cmat_ref[hf], preferred_element_type=F32)
         for hf in range(2)], axis=1)
    y = jax.nn.gelu(y + dskip_ref[...] * xi)
    gate = jnp.dot(y.astype(BF16), wglu_ref[...], preferred_element_type=F32) + bglu_ref[...]
    y = y * jax.nn.sigmoid(gate)
    for k in range(nslab):
        yi_scr[k] = y[:, k * LANES:(k + 1) * LANES]
    for b in range(nb):
        yb = jnp.concatenate([yi_scr[k, pl.ds(b, tt, stride=nb), :] for k in range(nslab)], axis=1)
        o_ref[b] = (yb * jax.nn.silu(ga_ref[b].astype(F32))).astype(o_ref.dtype)


def _s5(z3d, xa_blk, ga_blk, bmat, cmat, lam2, dskip, wglu, bglu, tt):
    nb, seq, _ = z3d.shape
    width = dskip.shape[-1]
    ncol = lam2.shape[-1]
    nslab = width // LANES
    kern = functools.partial(_s5_kernel, nb=nb, tt=tt)
    full = lambda a: pl.BlockSpec(a.shape, lambda i: (0,) * a.ndim)
    return pl.pallas_call(
        kern,
        grid=(seq // tt,),
        in_specs=[pl.BlockSpec((nb, tt, width), lambda i: (0, i, xa_blk)),
                  pl.BlockSpec((nb, tt, width), lambda i: (0, i, ga_blk)),
                  full(bmat), full(cmat), full(lam2), full(dskip), full(wglu), full(bglu)],
        out_specs=pl.BlockSpec((nb, tt, width), lambda i: (0, i, 0)),
        out_shape=jax.ShapeDtypeStruct((nb, seq, width), BF16),
        scratch_shapes=[pltpu.VMEM((nslab, nb * tt, LANES), F32),
                        pltpu.VMEM((ncol // LANES, nb * tt, LANES), F32),
                        pltpu.VMEM((nslab, nb * tt, LANES), F32),
                        pltpu.VMEM((SUBLANES, ncol), F32)],
        compiler_params=_params("arbitrary"),
        name="s5_mixer",
    )(z3d, z3d, bmat, cmat, lam2, dskip, wglu, bglu)


def _sgu_kernel(u_ref, v_ref, gb_ref, lng_ref, lnb_ref, ws_ref, bst_ref, o_ref):
    tm, width = u_ref.shape
    heads = width // SGU_HEAD_DIM
    vf = jax.nn.gelu(v_ref[...].astype(F32))
    mu = jnp.mean(vf, axis=-1, keepdims=True)
    var = jnp.mean(jnp.square(vf - mu), axis=-1, keepdims=True)
    vn = ((vf - mu) * lax.rsqrt(var + LN_EPS) * lng_ref[...] + lnb_ref[...]).astype(BF16)
    row = lax.broadcasted_iota(jnp.int32, (CHUNK, CHUNK), 0)
    col = lax.broadcasted_iota(jnp.int32, (CHUNK, CHUNK), 1)
    causal = col <= row
    for h in range(heads):
        ws = jnp.where(causal, ws_ref[h], 0.0).astype(BF16)
        bias = bst_ref[:, h:h + 1]
        cs = slice(h * SGU_HEAD_DIM, (h + 1) * SGU_HEAD_DIM)
        for c in range(tm // CHUNK):
            rs = slice(c * CHUNK, (c + 1) * CHUNK)
            s = jnp.dot(ws, vn[rs, cs], preferred_element_type=F32) + bias
            u = jax.nn.gelu(u_ref[rs, cs].astype(F32))
            gate = jax.nn.silu(gb_ref[rs, cs].astype(F32))
            o_ref[rs, cs] = (u * s * gate).astype(o_ref.dtype)


def _sgu(z2d, u_blk, v_blk, gb_blk, ln_g, ln_b, w_s, b_s_t, tm):
    n = z2d.shape[0]
    width = ln_g.shape[-1]
    full = lambda a: pl.BlockSpec(a.shape, lambda i: (0,) * a.ndim)
    return pl.pallas_call(
        _sgu_kernel,
        grid=(n // tm,),
        in_specs=[pl.BlockSpec((tm, width), lambda i: (i, u_blk)),
                  pl.BlockSpec((tm, width), lambda i: (i, v_blk)),
                  pl.BlockSpec((tm, width), lambda i: (i, gb_blk)),
                  full(ln_g), full(ln_b), full(w_s), full(b_s_t)],
        out_specs=pl.BlockSpec((tm, width), lambda i: (i, 0)),
        out_shape=jax.ShapeDtypeStruct((n, width), BF16),
        compiler_params=_params("parallel"),
        name="sgu_mixer",
    )(z2d, z2d, z2d, ln_g, ln_b, w_s, b_s_t)


def _pool_kernel(xc_ref, gc_ref, wp_ref, sc_ref, o_ref, ext_scr, *, tm):
    j = pl.program_id(1)

    @pl.when(j == 0)
    def _():
        ext_scr[0:POOL_HALO, :] = jnp.zeros((POOL_HALO, ext_scr.shape[-1]), F32)

    x = xc_ref[...].astype(F32)
    ext_scr[POOL_HALO:, :] = x
    pos = (lax.broadcasted_iota(jnp.int32, (tm, LANES), 0) + (j * tm + 1)).astype(F32)
    gch = wp_ref.shape[-1]
    for g, w in enumerate(POOL_WINDOWS):
        cs = slice(g * gch, (g + 1) * gch)
        acc = x[:, cs]
        for k in range(1, w):
            acc = acc + ext_scr[POOL_HALO - k:POOL_HALO - k + tm, cs]
        mean = acc / jnp.minimum(pos, float(w))
        p = (mean - x[:, cs]).astype(BF16)
        y = jnp.dot(p, wp_ref[g], preferred_element_type=F32) * sc_ref[:, cs]
        o_ref[:, cs] = (y * jax.nn.silu(gc_ref[:, cs].astype(F32))).astype(o_ref.dtype)
    ext_scr[0:POOL_HALO, :] = x[tm - POOL_HALO:, :]


def _pool(z2d, nb, xc_blk, gc_blk, w_pool_bf, scale, tm):
    n = z2d.shape[0]
    seq = n // nb
    width = scale.shape[-1]
    nt = seq // tm
    full = lambda a: pl.BlockSpec(a.shape, lambda b, j: (0,) * a.ndim)
    return pl.pallas_call(
        functools.partial(_pool_kernel, tm=tm),
        grid=(nb, nt),
        in_specs=[pl.BlockSpec((tm, width), lambda b, j: (b * nt + j, xc_blk)),
                  pl.BlockSpec((tm, width), lambda b, j: (b * nt + j, gc_blk)),
                  full(w_pool_bf), full(scale)],
        out_specs=pl.BlockSpec((tm, width), lambda b, j: (b * nt + j, 0)),
        out_shape=jax.ShapeDtypeStruct((n, width), BF16),
        scratch_shapes=[pltpu.VMEM((tm + POOL_HALO, width), F32)],
        compiler_params=_params("parallel", "arbitrary"),
        name="pool_mixer",
    )(z2d, z2d, w_pool_bf, scale)


def _outproj_kernel(x_ref, ya_ref, yb_ref, yc_ref, wa_ref, wb_ref, wc_ref, fg_ref, o_ref, *, final):
    acc = x_ref[...]
    acc = acc + jnp.dot(ya_ref[...], wa_ref[...], preferred_element_type=F32)
    acc = acc + jnp.dot(yb_ref[...], wb_ref[...], preferred_element_type=F32)
    acc = acc + jnp.dot(yc_ref[...], wc_ref[...], preferred_element_type=F32)
    if final:
        ms = jnp.mean(acc * acc, axis=-1, keepdims=True)
        acc = acc * lax.rsqrt(ms + RMS_EPS) * fg_ref[...]
    o_ref[...] = acc


def _outproj(x2d, ya, yb, yc, wa, wb, wc, final_g, final, tm):
    n, d = x2d.shape
    full = lambda a: pl.BlockSpec(a.shape, lambda i: (0,) * a.ndim)
    rows = lambda a: pl.BlockSpec((tm, a.shape[1]), lambda i: (i, 0))
    return pl.pallas_call(
        functools.partial(_outproj_kernel, final=final),
        grid=(n // tm,),
        in_specs=[rows(x2d), rows(ya), rows(yb), rows(yc), full(wa), full(wb), full(wc), full(final_g)],
        out_specs=pl.BlockSpec((tm, d), lambda i: (i, 0)),
        out_shape=jax.ShapeDtypeStruct((n, d), F32),
        compiler_params=_params("parallel"),
        name="outproj",
    )(x2d, ya, yb, yc, wa, wb, wc, final_g)


def _tile(n, pref):
    t = min(n, pref)
    assert n % t == 0
    return t


def kernel(x, norm_g, w_in, lam_re, lam_im, b_re, b_im, c_re, c_im, d_skip, log_dt, w_glu, b_glu,
           ln_g, ln_b, w_s, b_s, w_pool, pool_scale, w_out, final_g):
    nb, seq, d = x.shape
    depth = norm_g.shape[0]
    groups, state, gch = b_re.shape[1:]
    s5w = groups * gch
    sguw = ln_g.shape[-1]
    poolw = pool_scale.shape[-1]
    n = nb * seq
    assert s5w == 512 and sguw == 1024 and poolw == 512 and seq % CHUNK == 0

    o_xa, o_u, o_v = 0, s5w, s5w + sguw
    o_xc = o_v + sguw
    o_ga = o_xc + poolw
    o_gb = o_ga + s5w
    o_gc = o_gb + sguw
    perm = jnp.concatenate([jnp.arange(o_u, o_u + sguw), jnp.arange(o_v, o_v + sguw),
                            jnp.arange(o_gb, o_gb + sguw), jnp.arange(o_xa, o_xa + s5w),
                            jnp.arange(o_xc, o_xc + poolw), jnp.arange(o_ga, o_ga + s5w),
                            jnp.arange(o_gc, o_gc + poolw)])
    u_blk, v_blk, gb_blk = 0, 1, 2
    xa_blk, xc_blk, ga_blk, gc_blk = 6, 7, 8, 9

    tm_in = _tile(n, 1024)
    tn_in = 512
    tt = _tile(seq, 128)
    tm_sgu = _tile(n, 256)
    tm_pool = _tile(seq, 256)
    tm_out = _tile(n, 256)

    half_g = groups // 2
    x2d = x.reshape(n, d)
    for i in range(depth):
        w_perm = jnp.take(w_in[i], perm, axis=1).astype(BF16)
        z = _inproj(x2d, norm_g[i][None, :], w_perm, tm_in, tn_in)

        lbre, lbim, bbre, bbim = _s5_prep(lam_re[i], lam_im[i], log_dt[i], b_re[i], b_im[i])
        lbre = lbre[::gch]
        lbim = lbim[::gch]
        bmats, cmats, lam_a, lam_b = [], [], [], []
        ccre = jnp.transpose(c_re[i], (0, 2, 1))
        ccim = jnp.transpose(c_im[i], (0, 2, 1))
        for hf in range(2):
            gs = slice(hf * half_g, (hf + 1) * half_g)
            bd = lambda a: _block_diag(a[gs])
            bmats.append(jnp.concatenate([bd(bbre.reshape(groups, gch, state)),
                                          bd(bbim.reshape(groups, gch, state))], axis=1))
            cmats.append(jnp.concatenate([bd(ccre), -bd(ccim)], axis=0))
            lr = lbre[gs].reshape(-1)
            li = lbim[gs].reshape(-1)
            lam_a += [lr, lr]
            lam_b += [-li, li]
        bmat = jnp.stack(bmats).astype(BF16)
        cmat = jnp.stack(cmats).astype(BF16)
        lam2 = jnp.stack([jnp.concatenate(lam_a), jnp.concatenate(lam_b)])

        ya = _s5(z.reshape(nb, seq, -1), xa_blk, ga_blk, bmat, cmat, lam2,
                 d_skip[i].reshape(1, s5w), w_glu[i].astype(BF16), b_glu[i][None, :], tt)
        yb = _sgu(z, u_blk, v_blk, gb_blk, ln_g[i][None, :], ln_b[i][None, :], w_s[i],
                  jnp.transpose(b_s[i]), tm_sgu)
        yc = _pool(z, nb, xc_blk, gc_blk, w_pool[i].astype(BF16), pool_scale[i][None, :], tm_pool)

        wo = w_out[i].astype(BF16)
        x2d = _outproj(x2d, ya.reshape(n, s5w), yb, yc, wo[:s5w], wo[s5w:s5w + sguw], wo[s5w + sguw:],
                       final_g[None, :], i == depth - 1, tm_out)
    return x2d.reshape(nb, seq, d)
```

```python
import functools

import jax
import jax.numpy as jnp
from jax import lax
from jax.experimental import pallas as pl
from jax.experimental.pallas import tpu as pltpu

F32 = jnp.float32
BF16 = jnp.bfloat16

LANES = 128
SUBLANES = 8
VMEM_LIMIT_BYTES = 58 * 1024 * 1024

CHUNK = 128
SGU_HEAD_DIM = 128
POOL_WINDOWS = (2, 4, 8, 16)
POOL_HALO = 16
RMS_EPS = 1e-6
LN_EPS = 1e-5


def _params(*sem):
    return pltpu.CompilerParams(dimension_semantics=sem, vmem_limit_bytes=VMEM_LIMIT_BYTES)


def _resident(a):
    nd = a.ndim
    return pl.BlockSpec(a.shape, lambda *_: (0,) * nd, pipeline_mode=pl.Buffered(1))


def _s5_prep_kernel(lre_ref, lim_ref, ldt_ref, bre_ref, bim_ref,
                    lbre_ref, lbim_ref, bbre_ref, bbim_ref):
    lre = lre_ref[...]
    lim = lim_ref[...]
    dt = jnp.exp(ldt_ref[...])
    mag = jnp.exp(lre * dt)
    lbre = mag * jnp.cos(lim * dt)
    lbim = mag * jnp.sin(lim * dt)
    nre = lbre - 1.0
    nim = lbim
    den = lre * lre + lim * lim
    mre = (nre * lre + nim * lim) / den
    mim = (nim * lre - nre * lim) / den
    bre = bre_ref[...]
    bim = bim_ref[...]
    lbre_ref[...] = lbre
    lbim_ref[...] = lbim
    bbre_ref[...] = mre * bre - mim * bim
    bbim_ref[...] = mre * bim + mim * bre


def _s5_prep(lam_re, lam_im, log_dt, b_re, b_im):
    g, p, h = b_re.shape
    rep = lambda a: jnp.repeat(a, h, axis=0)
    lre = rep(lam_re)
    lim = rep(lam_im)
    ldt = jnp.broadcast_to(rep(log_dt[:, None]), (g * h, p))
    bre_t = jnp.transpose(b_re, (0, 2, 1)).reshape(g * h, p)
    bim_t = jnp.transpose(b_im, (0, 2, 1)).reshape(g * h, p)
    shp = jax.ShapeDtypeStruct((g * h, p), F32)
    return pl.pallas_call(_s5_prep_kernel, out_shape=(shp, shp, shp, shp),
                          name="s5_prep")(lre, lim, ldt, bre_t, bim_t)


def _block_diag(blocks):
    nb, r, c = blocks.shape
    eye = jnp.eye(nb, dtype=blocks.dtype)
    return (eye[:, None, :, None] * blocks[:, :, None, :]).reshape(nb * r, nb * c)


def _inproj_kernel(x_ref, g_ref, w_ref, z_ref, *, tn):
    x = x_ref[...]
    ms = jnp.mean(x * x, axis=-1, keepdims=True)
    h = (x * lax.rsqrt(ms + RMS_EPS) * g_ref[...]).astype(BF16)
    for j in range(z_ref.shape[-1] // tn):
        cs = slice(j * tn, (j + 1) * tn)
        z_ref[:, cs] = jnp.dot(h, w_ref[:, cs], preferred_element_type=F32).astype(z_ref.dtype)


def _inproj(x2d, gain, w_bf16, tm, tn):
    n, d = x2d.shape
    cols = w_bf16.shape[1]
    return pl.pallas_call(
        functools.partial(_inproj_kernel, tn=tn),
        grid=(n // tm,),
        in_specs=[pl.BlockSpec((tm, d), lambda i: (i, 0)), _resident(gain), _resident(w_bf16)],
        out_specs=pl.BlockSpec((tm, cols), lambda i: (i, 0)),
        out_shape=jax.ShapeDtypeStruct((n, cols), BF16),
        compiler_params=_params("parallel"),
        name="inproj",
    )(x2d, gain, w_bf16)


def _mixer_kernel(z_ref, x_ref, bmat_ref, cmat_ref, lama_ref, lamb_ref, dskip_ref, wglu_ref, bglu_ref,
                  lng_ref, lnb_ref, ws_ref, bst_ref, wp_ref, psc_ref, wo_ref, fg_ref,
                  o_ref, bu_scr, st_scr, ext_scr, *, nb, tt, s5w, sguw, poolw, final):
    i = pl.program_id(0)
    lane = lambda k: slice(k * LANES, (k + 1) * LANES)
    nst = bu_scr.shape[0]
    scol = nst * LANES
    half = s5w // 2
    nrow = 2 * nb
    o_xa, o_u, o_v = 0, s5w, s5w + sguw
    o_xc = o_v + sguw
    o_ga = o_xc + poolw
    o_gb = o_ga + s5w
    o_gc = o_gb + sguw

    @pl.when(i == 0)
    def _():
        st_scr[...] = jnp.zeros_like(st_scr)
        ext_scr[:, 0:POOL_HALO, :] = jnp.zeros((nb, POOL_HALO, poolw), F32)

    for b in range(nb):
        for hf in range(2):
            bu = jnp.dot(z_ref[b, :, o_xa + hf * half:o_xa + (hf + 1) * half], bmat_ref[hf],
                         preferred_element_type=F32)
            for k in range(nst):
                bu_scr[k, pl.ds(hf * nb + b, tt, stride=nrow), :] = bu[:, lane(k)]

    lam_a = lama_ref[...]
    lam_b = lamb_ref[...]

    def swap(h):
        return jnp.concatenate([h[:, scol // 2:], h[:, :scol // 2]], axis=1)

    def step(t, h):
        rows = pl.ds(pl.multiple_of(t * nrow, nrow), nrow)
        bu = jnp.concatenate([bu_scr[k, rows, :] for k in range(nst)], axis=1)
        hn = lam_a * h + lam_b * swap(h) + bu
        for k in range(nst):
            bu_scr[k, rows, :] = hn[:, lane(k)]
        return hn

    st_scr[...] = lax.fori_loop(0, tt, step, st_scr[...], unroll=4)

    row = lax.broadcasted_iota(jnp.int32, (CHUNK, CHUNK), 0)
    col = lax.broadcasted_iota(jnp.int32, (CHUNK, CHUNK), 1)
    causal = col <= row
    pos = (lax.broadcasted_iota(jnp.int32, (tt, LANES), 0) + (i * tt + 1)).astype(F32)
    gch = wp_ref.shape[-1]

    for b in range(nb):
        xa = z_ref[b, :, o_xa:o_xa + s5w].astype(F32)
        ys = []
        for hf in range(2):
            hs = jnp.concatenate([bu_scr[k, pl.ds(hf * nb + b, tt, stride=nrow), :] for k in range(nst)],
                                 axis=1).astype(BF16)
            ys.append(jnp.dot(hs, cmat_ref[hf], preferred_element_type=F32))
        y = jnp.concatenate(ys, axis=1)
        y = jax.nn.gelu(y + dskip_ref[...] * xa)
        gate = jnp.dot(y.astype(BF16), wglu_ref[...], preferred_element_type=F32) + bglu_ref[...]
        ya = y * jax.nn.sigmoid(gate) * jax.nn.silu(z_ref[b, :, o_ga:o_ga + s5w].astype(F32))
        acc = x_ref[b] + jnp.dot(ya.astype(BF16), wo_ref[0:s5w, :], preferred_element_type=F32)

        vf = jax.nn.gelu(z_ref[b, :, o_v:o_v + sguw].astype(F32))
        mu = jnp.mean(vf, axis=-1, keepdims=True)
        var = jnp.mean(jnp.square(vf - mu), axis=-1, keepdims=True)
        vn = ((vf - mu) * lax.rsqrt(var + LN_EPS) * lng_ref[...] + lnb_ref[...]).astype(BF16)
        ybs = []
        for h in range(sguw // SGU_HEAD_DIM):
            ws = jnp.where(causal, ws_ref[h], 0.0).astype(BF16)
            cs = slice(h * SGU_HEAD_DIM, (h + 1) * SGU_HEAD_DIM)
            parts = []
            for c in range(tt // CHUNK):
                rs = slice(c * CHUNK, (c + 1) * CHUNK)
                parts.append(jnp.dot(ws, vn[rs, cs], preferred_element_type=F32) + bst_ref[:, h:h + 1])
            ybs.append(parts[0] if len(parts) == 1 else jnp.concatenate(parts, axis=0))
        s = jnp.concatenate(ybs, axis=1)
        yb = (jax.nn.gelu(z_ref[b, :, o_u:o_u + sguw].astype(F32)) * s
              * jax.nn.silu(z_ref[b, :, o_gb:o_gb + sguw].astype(F32)))
        acc = acc + jnp.dot(yb.astype(BF16), wo_ref[s5w:s5w + sguw, :], preferred_element_type=F32)

        xc = z_ref[b, :, o_xc:o_xc + poolw].astype(F32)
        ext_scr[b, POOL_HALO:, :] = xc
        ycs = []
        for g, w in enumerate(POOL_WINDOWS):
            cs = slice(g * gch, (g + 1) * gch)
            win = xc[:, cs]
            for k in range(1, w):
                win = win + ext_scr[b, POOL_HALO - k:POOL_HALO - k + tt, cs]
            p = (win / jnp.minimum(pos, float(w)) - xc[:, cs]).astype(BF16)
            ycs.append(jnp.dot(p, wp_ref[g], preferred_element_type=F32))
        yc = (jnp.concatenate(ycs, axis=1) * psc_ref[...]
              * jax.nn.silu(z_ref[b, :, o_gc:o_gc + poolw].astype(F32)))
        ext_scr[b, 0:POOL_HALO, :] = xc[tt - POOL_HALO:, :]
        acc = acc + jnp.dot(yc.astype(BF16), wo_ref[s5w + sguw:, :], preferred_element_type=F32)

        if final:
            ms = jnp.mean(acc * acc, axis=-1, keepdims=True)
            acc = acc * lax.rsqrt(ms + RMS_EPS) * fg_ref[...]
        o_ref[b] = acc


def _mixers(z3d, x3d, weights, *, tt, s5w, sguw, poolw, final):
    nb, seq, cols = z3d.shape
    d = x3d.shape[-1]
    scol = weights[0].shape[-1]
    kern = functools.partial(_mixer_kernel, nb=nb, tt=tt, s5w=s5w, sguw=sguw, poolw=poolw, final=final)
    return pl.pallas_call(
        kern,
        grid=(seq // tt,),
        in_specs=[pl.BlockSpec((nb, tt, cols), lambda i: (0, i, 0)),
                  pl.BlockSpec((nb, tt, d), lambda i: (0, i, 0))] + [_resident(w) for w in weights],
        out_specs=pl.BlockSpec((nb, tt, d), lambda i: (0, i, 0)),
        out_shape=jax.ShapeDtypeStruct((nb, seq, d), F32),
        scratch_shapes=[pltpu.VMEM((scol // LANES, 2 * nb * tt, LANES), F32),
                        pltpu.VMEM((2 * nb, scol), F32),
                        pltpu.VMEM((nb, tt + POOL_HALO, poolw), F32)],
        compiler_params=_params("arbitrary"),
        name="mixers",
    )(z3d, x3d, *weights)


def _tile(n, pref):
    t = min(n, pref)
    assert n % t == 0
    return t


def kernel(x, norm_g, w_in, lam_re, lam_im, b_re, b_im, c_re, c_im, d_skip, log_dt, w_glu, b_glu,
           ln_g, ln_b, w_s, b_s, w_pool, pool_scale, w_out, final_g):
    nb, seq, d = x.shape
    depth = norm_g.shape[0]
    groups, state, gch = b_re.shape[1:]
    s5w = groups * gch
    sguw = ln_g.shape[-1]
    poolw = pool_scale.shape[-1]
    n = nb * seq
    assert nb * 2 == SUBLANES and seq % CHUNK == 0 and s5w % (2 * LANES) == 0

    tm_in = _tile(n, 512)
    tt = CHUNK

    half_g = groups // 2
    row = lambda a: a.reshape(1, -1)
    for i in range(depth):
        z = _inproj(x.reshape(n, d), row(norm_g[i]), w_in[i].astype(BF16), tm_in, 512)

        lbre, lbim, bbre, bbim = _s5_prep(lam_re[i], lam_im[i], log_dt[i], b_re[i], b_im[i])
        lbre = lbre[::gch]
        lbim = lbim[::gch]
        ccre = jnp.transpose(c_re[i], (0, 2, 1))
        ccim = jnp.transpose(c_im[i], (0, 2, 1))
        bmats, cmats, lam_a, lam_b = [], [], [], []
        for hf in range(2):
            gs = slice(hf * half_g, (hf + 1) * half_g)
            bd = lambda a: _block_diag(a[gs])
            bmats.append(jnp.concatenate([bd(bbre.reshape(groups, gch, state)),
                                          bd(bbim.reshape(groups, gch, state))], axis=1))
            cmats.append(jnp.concatenate([bd(ccre), -bd(ccim)], axis=0))
            lr = lbre[gs].reshape(1, -1)
            li = lbim[gs].reshape(1, -1)
            lam_a.append(jnp.broadcast_to(jnp.concatenate([lr, lr], axis=1), (nb, 2 * lr.shape[1])))
            lam_b.append(jnp.broadcast_to(jnp.concatenate([-li, li], axis=1), (nb, 2 * li.shape[1])))
        wo = w_out[i].astype(BF16)
        weights = [jnp.stack(bmats).astype(BF16),
                   jnp.stack(cmats).astype(BF16),
                   jnp.concatenate(lam_a, axis=0),
                   jnp.concatenate(lam_b, axis=0),
                   row(d_skip[i]), w_glu[i].astype(BF16), row(b_glu[i]),
                   row(ln_g[i]), row(ln_b[i]), w_s[i], jnp.transpose(b_s[i]),
                   w_pool[i].astype(BF16), row(pool_scale[i]), wo, row(final_g)]
        x = _mixers(z.reshape(nb, seq, -1), x, weights, tt=tt, s5w=s5w, sguw=sguw, poolw=poolw,
                    final=(i == depth - 1))
    return x
```

```python
import functools

import jax
import jax.numpy as jnp
from jax import lax
from jax.experimental import pallas as pl
from jax.experimental.pallas import tpu as pltpu

F32 = jnp.float32
BF16 = jnp.bfloat16

LANES = 128
SUBLANES = 8
VMEM_LIMIT_BYTES = 58 * 1024 * 1024

CHUNK = 128
SGU_HEAD_DIM = 128
POOL_WINDOWS = (2, 4, 8, 16)
POOL_HALO = 16
RMS_EPS = 1e-6
LN_EPS = 1e-5


def _params(*sem):
    return pltpu.CompilerParams(dimension_semantics=sem, vmem_limit_bytes=VMEM_LIMIT_BYTES)


def _resident(a):
    nd = a.ndim
    return pl.BlockSpec(a.shape, lambda *_: (0,) * nd, pipeline_mode=pl.Buffered(1))


def _s5_prep_kernel(lre_ref, lim_ref, ldt_ref, bre_ref, bim_ref,
                    lbre_ref, lbim_ref, bbre_ref, bbim_ref):
    lre = lre_ref[...]
    lim = lim_ref[...]
    dt = jnp.exp(ldt_ref[...])
    mag = jnp.exp(lre * dt)
    lbre = mag * jnp.cos(lim * dt)
    lbim = mag * jnp.sin(lim * dt)
    nre = lbre - 1.0
    nim = lbim
    den = lre * lre + lim * lim
    mre = (nre * lre + nim * lim) / den
    mim = (nim * lre - nre * lim) / den
    bre = bre_ref[...]
    bim = bim_ref[...]
    lbre_ref[...] = lbre
    lbim_ref[...] = lbim
    bbre_ref[...] = mre * bre - mim * bim
    bbim_ref[...] = mre * bim + mim * bre


def _s5_prep(lam_re, lam_im, log_dt, b_re, b_im):
    g, p, h = b_re.shape
    rep = lambda a: jnp.repeat(a, h, axis=0)
    lre = rep(lam_re)
    lim = rep(lam_im)
    ldt = jnp.broadcast_to(rep(log_dt[:, None]), (g * h, p))
    bre_t = jnp.transpose(b_re, (0, 2, 1)).reshape(g * h, p)
    bim_t = jnp.transpose(b_im, (0, 2, 1)).reshape(g * h, p)
    shp = jax.ShapeDtypeStruct((g * h, p), F32)
    return pl.pallas_call(_s5_prep_kernel, out_shape=(shp, shp, shp, shp),
                          name="s5_prep")(lre, lim, ldt, bre_t, bim_t)


def _block_diag(blocks):
    nb, r, c = blocks.shape
    eye = jnp.eye(nb, dtype=blocks.dtype)
    return (eye[:, None, :, None] * blocks[:, :, None, :]).reshape(nb * r, nb * c)


def _inproj_kernel(x_ref, g_ref, w_ref, z_ref, *, tn):
    x = x_ref[...]
    ms = jnp.mean(x * x, axis=-1, keepdims=True)
    h = (x * lax.rsqrt(ms + RMS_EPS) * g_ref[...]).astype(BF16)
    for j in range(z_ref.shape[-1] // tn):
        cs = slice(j * tn, (j + 1) * tn)
        z_ref[:, cs] = jnp.dot(h, w_ref[:, cs], preferred_element_type=F32).astype(z_ref.dtype)


def _inproj(x2d, gain, w_bf16, tm, tn):
    n, d = x2d.shape
    cols = w_bf16.shape[1]
    return pl.pallas_call(
        functools.partial(_inproj_kernel, tn=tn),
        grid=(n // tm,),
        in_specs=[pl.BlockSpec((tm, d), lambda i: (i, 0)), _resident(gain), _resident(w_bf16)],
        out_specs=pl.BlockSpec((tm, cols), lambda i: (i, 0)),
        out_shape=jax.ShapeDtypeStruct((n, cols), BF16),
        compiler_params=_params("parallel"),
        name="inproj",
    )(x2d, gain, w_bf16)


def _mixer_kernel(z_ref, x_ref, bmat_ref, cmat_ref, lama_ref, lamb_ref, dskip_ref, wglu_ref, bglu_ref,
                  lng_ref, lnb_ref, ws_ref, bst_ref, wp_ref, psc_ref, wo_ref, fg_ref,
                  o_ref, bu_scr, st_scr, ext_scr, ycat_scr, *, nb, tt, s5w, sguw, poolw, final):
    i = pl.program_id(0)
    lane = lambda k: slice(k * LANES, (k + 1) * LANES)
    nst = bu_scr.shape[0]
    scol = nst * LANES
    half = s5w // 2
    nrow = 2 * nb
    pitch = bu_scr.shape[1] // tt
    rows = nb * tt
    o_xa, o_u, o_v = 0, s5w, s5w + sguw
    o_xc = o_v + sguw
    o_ga = o_xc + poolw
    o_gb = o_ga + s5w
    o_gc = o_gb + sguw

    def zcols(start, width):
        return z_ref[:, :, start:start + width].reshape(rows, width)

    @pl.when(i == 0)
    def _():
        st_scr[...] = jnp.zeros_like(st_scr)
        ext_scr[:, 0:POOL_HALO, :] = jnp.zeros((nb, POOL_HALO, poolw), F32)

    for hf in range(2):
        bu = jnp.dot(zcols(o_xa + hf * half, half), bmat_ref[hf], preferred_element_type=F32)
        for b in range(nb):
            for k in range(nst):
                bu_scr[k, pl.ds(hf * nb + b, tt, stride=pitch), :] = bu[b * tt:(b + 1) * tt, lane(k)]

    lam_a = lama_ref[...]
    lam_b = lamb_ref[...]

    def swap(h):
        return jnp.concatenate([h[:, scol // 2:], h[:, :scol // 2]], axis=1)

    def step(t, h):
        r8 = pl.ds(t * pitch, nrow)
        bu = jnp.concatenate([bu_scr[k, r8, :] for k in range(nst)], axis=1)
        hn = lam_a * h + lam_b * swap(h) + bu
        for k in range(nst):
            bu_scr[k, r8, :] = hn[:, lane(k)]
        return hn

    st_scr[...] = lax.fori_loop(0, tt, step, st_scr[...], unroll=4)

    ys = []
    for hf in range(2):
        hs = jnp.concatenate(
            [jnp.concatenate([bu_scr[k, pl.ds(hf * nb + b, tt, stride=pitch), :] for k in range(nst)],
                             axis=1).astype(BF16) for b in range(nb)], axis=0)
        ys.append(jnp.dot(hs, cmat_ref[hf], preferred_element_type=F32))
    y = jnp.concatenate(ys, axis=1)
    y = jax.nn.gelu(y + dskip_ref[...] * zcols(o_xa, s5w).astype(F32))
    gate = jnp.dot(y.astype(BF16), wglu_ref[...], preferred_element_type=F32) + bglu_ref[...]
    ya = y * jax.nn.sigmoid(gate) * jax.nn.silu(zcols(o_ga, s5w).astype(F32))
    ycat_scr[:, 0:s5w] = ya.astype(BF16)

    row = lax.broadcasted_iota(jnp.int32, (CHUNK, CHUNK), 0)
    col = lax.broadcasted_iota(jnp.int32, (CHUNK, CHUNK), 1)
    causal = col <= row
    vf = jax.nn.gelu(zcols(o_v, sguw).astype(F32))
    mu = jnp.mean(vf, axis=-1, keepdims=True)
    var = jnp.mean(jnp.square(vf - mu), axis=-1, keepdims=True)
    vn = ((vf - mu) * lax.rsqrt(var + LN_EPS) * lng_ref[...] + lnb_ref[...]).astype(BF16)
    for h in range(sguw // SGU_HEAD_DIM):
        ws = jnp.where(causal, ws_ref[h], 0.0).astype(BF16)
        cs = slice(h * SGU_HEAD_DIM, (h + 1) * SGU_HEAD_DIM)
        s = jnp.concatenate(
            [jnp.dot(ws, vn[c * CHUNK:(c + 1) * CHUNK, cs], preferred_element_type=F32)
             for c in range(rows // CHUNK)], axis=0) + jnp.tile(bst_ref[:, h:h + 1], (rows // CHUNK, 1))
        yb = (jax.nn.gelu(zcols(o_u + h * SGU_HEAD_DIM, SGU_HEAD_DIM).astype(F32)) * s
              * jax.nn.silu(zcols(o_gb + h * SGU_HEAD_DIM, SGU_HEAD_DIM).astype(F32)))
        ycat_scr[:, s5w + h * SGU_HEAD_DIM:s5w + (h + 1) * SGU_HEAD_DIM] = yb.astype(BF16)

    pos = (lax.broadcasted_iota(jnp.int32, (tt, LANES), 0) + (i * tt + 1)).astype(F32)
    gch = wp_ref.shape[-1]
    for b in range(nb):
        ext_scr[b, POOL_HALO:, :] = z_ref[b, :, o_xc:o_xc + poolw].astype(F32)
    for g, w in enumerate(POOL_WINDOWS):
        cs = slice(g * gch, (g + 1) * gch)
        ps = []
        for b in range(nb):
            cur = ext_scr[b, POOL_HALO:, cs]
            win = cur
            for k in range(1, w):
                win = win + ext_scr[b, POOL_HALO - k:POOL_HALO - k + tt, cs]
            ps.append((win / jnp.minimum(pos, float(w)) - cur).astype(BF16))
        yc = (jnp.dot(jnp.concatenate(ps, axis=0), wp_ref[g], preferred_element_type=F32) * psc_ref[:, cs]
              * jax.nn.silu(zcols(o_gc + g * gch, gch).astype(F32)))
        ycat_scr[:, s5w + sguw + g * gch:s5w + sguw + (g + 1) * gch] = yc.astype(BF16)
    for b in range(nb):
        ext_scr[b, 0:POOL_HALO, :] = ext_scr[b, tt:tt + POOL_HALO, :]

    acc = x_ref[...].reshape(rows, -1) + jnp.dot(ycat_scr[...], wo_ref[...], preferred_element_type=F32)
    if final:
        ms = jnp.mean(acc * acc, axis=-1, keepdims=True)
        acc = acc * lax.rsqrt(ms + RMS_EPS) * fg_ref[...]
    o_ref[...] = acc.reshape(o_ref.shape)


def _mixers(z3d, x3d, weights, *, tt, s5w, sguw, poolw, final):
    nb, seq, cols = z3d.shape
    d = x3d.shape[-1]
    scol = weights[0].shape[-1]
    kern = functools.partial(_mixer_kernel, nb=nb, tt=tt, s5w=s5w, sguw=sguw, poolw=poolw, final=final)
    return pl.pallas_call(
        kern,
        grid=(seq // tt,),
        in_specs=[pl.BlockSpec((nb, tt, cols), lambda i: (0, i, 0)),
                  pl.BlockSpec((nb, tt, d), lambda i: (0, i, 0))] + [_resident(w) for w in weights],
        out_specs=pl.BlockSpec((nb, tt, d), lambda i: (0, i, 0)),
        out_shape=jax.ShapeDtypeStruct((nb, seq, d), F32),
        scratch_shapes=[pltpu.VMEM((scol // LANES, (2 * nb + 1) * tt, LANES), F32),
                        pltpu.VMEM((2 * nb, scol), F32),
                        pltpu.VMEM((nb, tt + POOL_HALO, poolw), F32),
                        pltpu.VMEM((nb * tt, s5w + sguw + poolw), BF16)],
        compiler_params=_params("arbitrary"),
        name="mixers",
    )(z3d, x3d, *weights)


def _tile(n, pref):
    t = min(n, pref)
    assert n % t == 0
    return t


def kernel(x, norm_g, w_in, lam_re, lam_im, b_re, b_im, c_re, c_im, d_skip, log_dt, w_glu, b_glu,
           ln_g, ln_b, w_s, b_s, w_pool, pool_scale, w_out, final_g):
    nb, seq, d = x.shape
    depth = norm_g.shape[0]
    groups, state, gch = b_re.shape[1:]
    s5w = groups * gch
    sguw = ln_g.shape[-1]
    poolw = pool_scale.shape[-1]
    n = nb * seq
    assert nb * 2 == SUBLANES and seq % CHUNK == 0 and s5w % (2 * LANES) == 0

    tm_in = _tile(n, 512)
    tt = CHUNK

    half_g = groups // 2
    row = lambda a: a.reshape(1, -1)
    for i in range(depth):
        z = _inproj(x.reshape(n, d), row(norm_g[i]), w_in[i].astype(BF16), tm_in, 512)

        lbre, lbim, bbre, bbim = _s5_prep(lam_re[i], lam_im[i], log_dt[i], b_re[i], b_im[i])
        lbre = lbre[::gch]
        lbim = lbim[::gch]
        ccre = jnp.transpose(c_re[i], (0, 2, 1))
        ccim = jnp.transpose(c_im[i], (0, 2, 1))
        bmats, cmats, lam_a, lam_b = [], [], [], []
        for hf in range(2):
            gs = slice(hf * half_g, (hf + 1) * half_g)
            bd = lambda a: _block_diag(a[gs])
            bmats.append(jnp.concatenate([bd(bbre.reshape(groups, gch, state)),
                                          bd(bbim.reshape(groups, gch, state))], axis=1))
            cmats.append(jnp.concatenate([bd(ccre), -bd(ccim)], axis=0))
            lr = lbre[gs].reshape(1, -1)
            li = lbim[gs].reshape(1, -1)
            lam_a.append(jnp.broadcast_to(jnp.concatenate([lr, lr], axis=1), (nb, 2 * lr.shape[1])))
            lam_b.append(jnp.broadcast_to(jnp.concatenate([-li, li], axis=1), (nb, 2 * li.shape[1])))
        wo = w_out[i].astype(BF16)
        weights = [jnp.stack(bmats).astype(BF16),
                   jnp.stack(cmats).astype(BF16),
                   jnp.concatenate(lam_a, axis=0),
                   jnp.concatenate(lam_b, axis=0),
                   row(d_skip[i]), w_glu[i].astype(BF16), row(b_glu[i]),
                   row(ln_g[i]), row(ln_b[i]), w_s[i], jnp.transpose(b_s[i]),
                   w_pool[i].astype(BF16), row(pool_scale[i]), wo, row(final_g)]
        x = _mixers(z.reshape(nb, seq, -1), x, weights, tt=tt, s5w=s5w, sguw=sguw, poolw=poolw,
                    final=(i == depth - 1))
    return x
```
